```python
import math
import jax, jax.numpy as jnp
from jax import lax
import numpy as np

D_MODEL = 1024
BATCH = 32
SEQ = 2048
DEPTH = 4

MEM_LEN = 256
EPS = 1e-6
ROPE_THETA = 500000.0
Q_BLOCK = 128

DIFF_HEADS = 8
DIFF_HEAD_DIM = 64
DIFF_V_DIM = 2 * DIFF_HEAD_DIM
DIFF_ROT_DIM = DIFF_HEAD_DIM // 4
DIFF_QK_WIDTH = DIFF_HEADS * 2 * DIFF_HEAD_DIM
DIFF_V_WIDTH = DIFF_HEADS * DIFF_V_DIM
DIFF_LAMBDA_STD = 0.1

MLA_HEADS = 8
MLA_Q_LORA = 512
MLA_KV_LORA = 256
MLA_NOPE_DIM = 64
MLA_ROPE_DIM = 32
MLA_V_DIM = 64
MLA_QK_DIM = MLA_NOPE_DIM + MLA_ROPE_DIM
MLA_V_WIDTH = MLA_HEADS * MLA_V_DIM

CROSS_HEADS = 4
CROSS_HEAD_DIM = 128
CROSS_WIDTH = CROSS_HEADS * CROSS_HEAD_DIM

N_BRANCHES = 3
FFN_HIDDEN = -(-8 * D_MODEL // (3 * 256)) * 256

IN_SIZES = (DIFF_QK_WIDTH, DIFF_QK_WIDTH, DIFF_V_WIDTH, MLA_Q_LORA, MLA_KV_LORA, MLA_ROPE_DIM, CROSS_WIDTH, N_BRANCHES * D_MODEL)
IN_COLS = sum(IN_SIZES)
IN_SPLIT_POINTS = tuple(sum(IN_SIZES[:j + 1]) for j in range(len(IN_SIZES) - 1))

kernel_name = "hybrid_diffattn_mla_memxattn_gated_trunk"


def rms_norm(x, w):
    xf = x.astype(jnp.float32)
    y = xf * lax.rsqrt(jnp.mean(xf * xf, axis=-1, keepdims=True) + EPS)
    return (y * w.astype(jnp.float32)).astype(x.dtype)


def apply_rope(x, positions, rot_dim):
    half = rot_dim // 2
    inv_freq = ROPE_THETA ** (-jnp.arange(half, dtype=jnp.float32) * 2.0 / rot_dim)
    ang = positions.astype(jnp.float32)[..., None] * inv_freq
    cos = jnp.cos(ang)[:, :, None, :]
    sin = jnp.sin(ang)[:, :, None, :]
    xf = x.astype(jnp.float32)
    x1 = xf[..., :half]
    x2 = xf[..., half:rot_dim]
    out = jnp.concatenate([x1 * cos - x2 * sin, x2 * cos + x1 * sin, xf[..., rot_dim:]], axis=-1)
    return out.astype(x.dtype)


def causal_block_probs(q_blk, k_pre, q_start, scale):
    n_q = q_blk.shape[1]
    n_k = k_pre.shape[1]
    s = jnp.einsum('bqhd,bkhd->bhqk', q_blk, k_pre).astype(jnp.float32) * scale
    mask = (q_start + jnp.arange(n_q))[:, None] >= jnp.arange(n_k)[None, :]
    s = jnp.where(mask, s, -jnp.inf)
    return jax.nn.softmax(s, axis=-1)


def diff_attention(q, k, v, lam, scale):
    seq = q.shape[1]
    outs = []
    for start in range(0, seq, Q_BLOCK):
        end = start + Q_BLOCK
        p1 = causal_block_probs(q[:, start:end, :, 0], k[:, :end, :, 0], start, scale)
        p2 = causal_block_probs(q[:, start:end, :, 1], k[:, :end, :, 1], start, scale)
        p = (p1 - lam * p2).astype(v.dtype)
        outs.append(jnp.einsum('bhqk,bkhd->bqhd', p, v[:, :end]))
    return jnp.concatenate(outs, axis=1)


def causal_attention(q, k, v, scale):
    seq = q.shape[1]
    outs = []
    for start in range(0, seq, Q_BLOCK):
        end = start + Q_BLOCK
        p = causal_block_probs(q[:, start:end], k[:, :end], start, scale).astype(v.dtype)
        outs.append(jnp.einsum('bhqk,bkhd->bqhd', p, v[:, :end]))
    return jnp.concatenate(outs, axis=1)


def setup_inputs(seed: int = 0) -> dict:
    key = jax.random.key(seed)
    ks = iter(jax.random.split(key, 32))

    def w(shape):
        return jax.random.normal(next(ks), shape, jnp.float32) * (shape[-2] ** -0.5)

    def gain(shape):
        return 1.0 + 0.02 * jax.random.normal(next(ks), shape, jnp.float32)

    x = jax.random.normal(next(ks), (BATCH, SEQ, D_MODEL), jnp.float32)
    mem = jax.random.normal(next(ks), (BATCH, MEM_LEN, D_MODEL), jnp.float32)
    offsets = jax.random.randint(next(ks), (BATCH, 1), 0, 1024, dtype=jnp.int32)
    positions = offsets + jnp.arange(SEQ, dtype=jnp.int32)[None, :]
    return {
        "x": x,
        "mem": mem,
        "positions": positions,
        "attn_norm_w": gain((DEPTH, D_MODEL)),
        "w_in": w((DEPTH, D_MODEL, IN_COLS)),
        "diff_lambda": DIFF_LAMBDA_STD * jax.random.normal(next(ks), (DEPTH, 4, DIFF_HEAD_DIM), jnp.float32),
        "diff_subln_w": gain((DEPTH, DIFF_V_DIM)),
        "w_o_diff": w((DEPTH, DIFF_V_WIDTH, D_MODEL)),
        "mla_q_norm_w": gain((DEPTH, MLA_Q_LORA)),
        "mla_w_uq": w((DEPTH, MLA_Q_LORA, MLA_HEADS * MLA_QK_DIM)),
        "mla_kv_norm_w": gain((DEPTH, MLA_KV_LORA)),
        "mla_w_ukv": w((DEPTH, MLA_KV_LORA, MLA_HEADS * (MLA_NOPE_DIM + MLA_V_DIM))),
        "w_o_mla": w((DEPTH, MLA_V_WIDTH, D_MODEL)),
        "mem_norm_w": gain((DEPTH, D_MODEL)),
        "w_mem_kv": w((DEPTH, D_MODEL, 2 * CROSS_WIDTH)),
        "w_o_cross": w((DEPTH, CROSS_WIDTH, D_MODEL)),
        "w_out": w((DEPTH, D_MODEL, D_MODEL)),
        "ffn_norm_w": gain((DEPTH, D_MODEL)),
        "w_gate_up": w((DEPTH, D_MODEL, 2 * FFN_HIDDEN)),
        "w_down": w((DEPTH, FFN_HIDDEN, D_MODEL)),
        "final_norm_w": gain((D_MODEL,)),
    }


def reference(x, mem, positions, attn_norm_w, w_in, diff_lambda, diff_subln_w, w_o_diff,
              mla_q_norm_w, mla_w_uq, mla_kv_norm_w, mla_w_ukv, w_o_mla,
              mem_norm_w, w_mem_kv, w_o_cross, w_out, ffn_norm_w, w_gate_up, w_down,
              final_norm_w):
    b, s, _ = x.shape
    m_len = mem.shape[1]
    for i in range(DEPTH):
        h = rms_norm(x, attn_norm_w[i])
        proj = h @ w_in[i]
        dq, dk, dv, cq, ckv, krope, xq, gates = jnp.split(proj, IN_SPLIT_POINTS, axis=-1)

        dq = apply_rope(dq.reshape(b, s, 2 * DIFF_HEADS, DIFF_HEAD_DIM), positions, DIFF_ROT_DIM)
        dk = apply_rope(dk.reshape(b, s, 2 * DIFF_HEADS, DIFF_HEAD_DIM), positions, DIFF_ROT_DIM)
        dq = dq.reshape(b, s, DIFF_HEADS, 2, DIFF_HEAD_DIM)
        dk = dk.reshape(b, s, DIFF_HEADS, 2, DIFF_HEAD_DIM)
        dv = dv.reshape(b, s, DIFF_HEADS, DIFF_V_DIM)
        lam_init = 0.8 - 0.6 * math.exp(-0.3 * i)
        lv = diff_lambda[i].astype(jnp.float32)
        lam = jnp.exp(jnp.sum(lv[0] * lv[1])) - jnp.exp(jnp.sum(lv[2] * lv[3])) + lam_init
        a = diff_attention(dq, dk, dv, lam, DIFF_HEAD_DIM ** -0.5)
        a = rms_norm(a, diff_subln_w[i]) * (1.0 - lam_init)
        y_a = a.reshape(b, s, DIFF_V_WIDTH) @ w_o_diff[i]

        qm = (rms_norm(cq, mla_q_norm_w[i]) @ mla_w_uq[i]).reshape(b, s, MLA_HEADS, MLA_QK_DIM)
        q_nope = qm[..., :MLA_NOPE_DIM]
        q_pe = apply_rope(qm[..., MLA_NOPE_DIM:], positions, MLA_ROPE_DIM)
        kv = (rms_norm(ckv, mla_kv_norm_w[i]) @ mla_w_ukv[i]).reshape(b, s, MLA_HEADS, MLA_NOPE_DIM + MLA_V_DIM)
        k_nope = kv[..., :MLA_NOPE_DIM]
        v_m = kv[..., MLA_NOPE_DIM:]
        k_pe = apply_rope(krope[:, :, None, :], positions, MLA_ROPE_DIM)
        q_full = jnp.concatenate([q_nope, q_pe], axis=-1)
        k_full = jnp.concatenate([k_nope, jnp.broadcast_to(k_pe, (b, s, MLA_HEADS, MLA_ROPE_DIM))], axis=-1)
        mo = causal_attention(q_full, k_full, v_m, MLA_QK_DIM ** -0.5)
        y_b = mo.reshape(b, s, MLA_V_WIDTH) @ w_o_mla[i]

        mh = rms_norm(mem, mem_norm_w[i])
        mk, mv = jnp.split(mh @ w_mem_kv[i], 2, axis=-1)
        mk = mk.reshape(b, m_len, CROSS_HEADS, CROSS_HEAD_DIM)
        mv = mv.reshape(b, m_len, CROSS_HEADS, CROSS_HEAD_DIM)
        xq = xq.reshape(b, s, CROSS_HEADS, CROSS_HEAD_DIM)
        sc = jnp.einsum('bqhd,bkhd->bhqk', xq, mk).astype(jnp.float32) * (CROSS_HEAD_DIM ** -0.5)
        pc = jax.nn.softmax(sc, axis=-1).astype(mv.dtype)
        co = jnp.einsum('bhqk,bkhd->bqhd', pc, mv)
        y_c = co.reshape(b, s, CROSS_WIDTH) @ w_o_cross[i]

        g_a, g_b, g_c = jnp.split(jax.nn.sigmoid(gates), N_BRANCHES, axis=-1)
        x = x + (g_a * y_a + g_b * y_b + g_c * y_c) @ w_out[i]

        h2 = rms_norm(x, ffn_norm_w[i])
        f_gate, f_up = jnp.split(h2 @ w_gate_up[i], 2, axis=-1)
        x = x + (jax.nn.silu(f_gate) * f_up) @ w_down[i]
    return rms_norm(x, final_norm_w)
```

```python
import functools
import math

import jax
import jax.numpy as jnp
from jax import lax
from jax.experimental import pallas as pl
from jax.experimental.pallas import tpu as pltpu

F32 = jnp.float32
BF16 = jnp.bfloat16

D_MODEL = 1024
EPS = 1e-6
ROPE_THETA = 500000.0

DIFF_HEADS = 8
DIFF_HEAD_DIM = 64
DIFF_ROT_DIM = DIFF_HEAD_DIM // 4
DIFF_WIDTH = 1024

MLA_HEADS = 8
MLA_Q_LORA = 512
MLA_KV_LORA = 256
MLA_NOPE_DIM = 64
MLA_ROPE_DIM = 32
MLA_V_DIM = 64
MLA_QK_DIM = MLA_NOPE_DIM + MLA_ROPE_DIM
MLA_V_WIDTH = MLA_HEADS * MLA_V_DIM

CROSS_HEADS = 4
CROSS_HEAD_DIM = 128
CROSS_WIDTH = CROSS_HEADS * CROSS_HEAD_DIM

FFN_HIDDEN = 2816

LANES = 128
CHUNK = 512

COL_GATES = 0
COL_DQ = 3072
COL_DK = 4096
COL_DV = 5120
COL_CQ = 6144
COL_CKV = 6656
COL_XQ = 7168
IN_COLS_PAD = 7680

VMEM_LIMIT = 56 * 1024 * 1024


def _cparams(sem):
    return pltpu.CompilerParams(dimension_semantics=sem, vmem_limit_bytes=VMEM_LIMIT)


def _resident(shape):
    nd = len(shape)
    return pl.BlockSpec(shape, lambda *_: (0,) * nd, pipeline_mode=pl.Buffered(1))


def _rms(x, w):
    y = x * lax.rsqrt(jnp.mean(x * x, axis=-1, keepdims=True) + EPS)
    return y * w


def _rope_table_kernel(pos_ref, freq_ref, sa_ref, sb_ref, c_ref, a_ref, b_ref):
    ang = pos_ref[...] * freq_ref[...]
    c = jnp.cos(ang)
    s = jnp.sin(ang)
    c_ref[...] = c
    a_ref[...] = s * sa_ref[...]
    b_ref[...] = s * sb_ref[...]


def _rope_tables(posf, freq, sign_a, sign_b, tm):
    t = posf.shape[0]
    tok = pl.BlockSpec((tm, LANES), lambda i: (i, 0))
    row = pl.BlockSpec((1, LANES), lambda i: (0, 0))
    out = jax.ShapeDtypeStruct((t, LANES), F32)
    return pl.pallas_call(
        _rope_table_kernel,
        grid=(t // tm,),
        in_specs=[pl.BlockSpec((tm, 1), lambda i: (i, 0)), row, row, row],
        out_specs=[tok, tok, tok],
        out_shape=[out, out, out],
        compiler_params=_cparams(("parallel",)),
        name="rope_tables",
    )(posf, freq, sign_a, sign_b)


def _rope128(x, c, a, b, half):
    return x * c + pltpu.roll(x, LANES - half, 1) * a + pltpu.roll(x, half, 1) * b


def _inproj_kernel(x_ref, g_ref, w_ref, c8_ref, a8_ref, b8_ref, c16_ref, a16_ref, b16_ref,
                   qn_ref, kvn_ref, o_ref, h_ref):
    h_ref[...] = _rms(x_ref[...], g_ref[...]).astype(BF16)
    for c0 in range(0, IN_COLS_PAD, CHUNK):
        acc = jnp.dot(h_ref[...], w_ref[:, c0:c0 + CHUNK], preferred_element_type=F32)
        if c0 < COL_DQ:
            o_ref[:, c0:c0 + CHUNK] = jax.nn.sigmoid(acc).astype(BF16)
        elif c0 < COL_DV:
            scale = DIFF_HEAD_DIM ** -0.5 if c0 < COL_DK else 1.0
            for s0 in range(0, CHUNK, LANES):
                r = _rope128(acc[:, s0:s0 + LANES], c8_ref[...], a8_ref[...], b8_ref[...],
                             DIFF_ROT_DIM // 2)
                o_ref[:, c0 + s0:c0 + s0 + LANES] = (r * scale).astype(BF16)
        elif c0 == COL_CQ:
            o_ref[:, c0:c0 + CHUNK] = _rms(acc, qn_ref[...]).astype(BF16)
        elif c0 == COL_CKV:
            o_ref[:, c0:c0 + MLA_KV_LORA] = _rms(acc[:, :MLA_KV_LORA], kvn_ref[...]).astype(BF16)
            kr = _rope128(acc[:, MLA_KV_LORA:MLA_KV_LORA + LANES], c16_ref[...], a16_ref[...],
                          b16_ref[...], MLA_ROPE_DIM // 2)
            o_ref[:, c0 + MLA_KV_LORA:c0 + MLA_KV_LORA + LANES] = kr.astype(BF16)
            o_ref[:, c0 + MLA_KV_LORA + LANES:c0 + CHUNK] = acc[:, MLA_KV_LORA + LANES:].astype(BF16)
        elif c0 == COL_XQ:
            o_ref[:, c0:c0 + CHUNK] = (acc * (CROSS_HEAD_DIM ** -0.5)).astype(BF16)
        else:
            o_ref[:, c0:c0 + CHUNK] = acc.astype(BF16)


def _inproj(x2, gain, w, tabs8, tabs16, qn, kvn, tm):
    t = x2.shape[0]
    tok = pl.BlockSpec((tm, LANES), lambda i: (i, 0))
    return pl.pallas_call(
        _inproj_kernel,
        grid=(t // tm,),
        in_specs=[pl.BlockSpec((tm, D_MODEL), lambda i: (i, 0)),
                  _resident((1, D_MODEL)),
                  _resident((D_MODEL, IN_COLS_PAD)),
                  tok, tok, tok, tok, tok, tok,
                  _resident((1, MLA_Q_LORA)),
                  _resident((1, MLA_KV_LORA))],
        out_specs=pl.BlockSpec((tm, IN_COLS_PAD), lambda i: (i, 0)),
        out_shape=jax.ShapeDtypeStruct((t, IN_COLS_PAD), BF16),
        scratch_shapes=[pltpu.VMEM((tm, D_MODEL), BF16)],
        compiler_params=_cparams(("parallel",)),
        name="inproj",
    )(x2, gain, w, *tabs8, *tabs16, qn, kvn)


def _mla_up_kernel(cq_ref, ckv_ref, wq_ref, wk_ref, wv_ref, c_ref, a_ref, b_ref,
                   q_ref, k_ref, v_ref):
    scale = MLA_QK_DIM ** -0.5
    q = jnp.dot(cq_ref[...], wq_ref[...], preferred_element_type=F32)
    lat = ckv_ref[:, :MLA_KV_LORA]
    kr = ckv_ref[:, MLA_KV_LORA:MLA_KV_LORA + LANES]
    k = jnp.dot(lat, wk_ref[...], preferred_element_type=F32)
    for hd in range(MLA_HEADS):
        sl = slice(hd * LANES, (hd + 1) * LANES)
        qh = _rope128(q[:, sl], c_ref[...], a_ref[...], b_ref[...], MLA_ROPE_DIM // 2)
        q_ref[:, sl] = (qh * scale).astype(BF16)
        k_ref[:, sl] = (k[:, sl] + kr.astype(F32)).astype(BF16)
    v_ref[...] = jnp.dot(lat, wv_ref[...], preferred_element_type=F32).astype(BF16)


def _mla_up(proj, wq, wk, wv, tabs16, tm):
    t = proj.shape[0]
    tok = pl.BlockSpec((tm, LANES), lambda i: (i, 0))
    wide = MLA_HEADS * LANES
    return pl.pallas_call(
        _mla_up_kernel,
        grid=(t // tm,),
        in_specs=[pl.BlockSpec((tm, CHUNK), lambda i: (i, COL_CQ // CHUNK)),
                  pl.BlockSpec((tm, CHUNK), lambda i: (i, COL_CKV // CHUNK)),
                  _resident((MLA_Q_LORA, wide)),
                  _resident((MLA_KV_LORA, wide)),
                  _resident((MLA_KV_LORA, MLA_V_WIDTH)),
                  tok, tok, tok],
        out_specs=[pl.BlockSpec((tm, wide), lambda i: (i, 0)),
                   pl.BlockSpec((tm, wide), lambda i: (i, 0)),
                   pl.BlockSpec((tm, MLA_V_WIDTH), lambda i: (i, 0))],
        out_shape=[jax.ShapeDtypeStruct((t, wide), BF16),
                   jax.ShapeDtypeStruct((t, wide), BF16),
                   jax.ShapeDtypeStruct((t, MLA_V_WIDTH), BF16)],
        compiler_params=_cparams(("parallel",)),
        name="mla_up",
    )(proj, proj, wq, wk, wv, *tabs16)


def _flash_causal(q_ref, k_at, v_at, m_ref, l_ref, acc_ref, qi, tq):
    rows = q_ref.shape[0]
    m_ref[...] = jnp.full(m_ref.shape, -1e30, F32)
    l_ref[...] = jnp.zeros(l_ref.shape, F32)
    acc_ref[...] = jnp.zeros(acc_ref.shape, F32)

    def block(j, masked):
        start = pl.multiple_of(j * tq, tq)
        s = lax.dot_general(q_ref[...], k_at(start), (((1,), (1,)), ((), ())),
                            preferred_element_type=F32)
        if masked:
            r = lax.broadcasted_iota(jnp.int32, (rows, tq), 0) & (tq - 1)
            c = lax.broadcasted_iota(jnp.int32, (rows, tq), 1)
            s = jnp.where(r >= c, s, -1e30)
        m_prev = m_ref[...]
        m_new = jnp.maximum(m_prev, jnp.max(s, axis=1, keepdims=True))
        alpha = jnp.exp(m_prev - m_new)
        p = jnp.exp(s - m_new)
        l_ref[...] = alpha * l_ref[...] + jnp.sum(p, axis=1, keepdims=True)
        acc_ref[...] = alpha * acc_ref[...] + jnp.dot(p.astype(BF16), v_at(start),
                                                      preferred_element_type=F32)
        m_ref[...] = m_new

    def body(j, carry):
        block(j, False)
        return carry

    lax.fori_loop(0, qi, body, 0)
    block(qi, True)


def _diff_attn_kernel(lam_ref, subw_ref, q_ref, k_ref, v_ref, o_ref,
                      q2_ref, m_ref, l_ref, acc_ref, *, tq, lam_init):
    qi = pl.program_id(2)
    q = q_ref[0]
    lane = lax.broadcasted_iota(jnp.int32, q.shape, 1)
    zero = jnp.zeros_like(q)
    q2_ref[0:tq, :] = jnp.where(lane < DIFF_HEAD_DIM, q, zero)
    q2_ref[tq:2 * tq, :] = jnp.where(lane >= DIFF_HEAD_DIM, q, zero)
    _flash_causal(q2_ref,
                  lambda st: k_ref[0, pl.ds(st, tq), :],
                  lambda st: v_ref[0, pl.ds(st, tq), :],
                  m_ref, l_ref, acc_ref, qi, tq)
    lv = lam_ref[...]
    lam = (jnp.exp(jnp.sum(lv[0:1] * lv[1:2], axis=1, keepdims=True))
           - jnp.exp(jnp.sum(lv[2:3] * lv[3:4], axis=1, keepdims=True)) + lam_init)
    o = acc_ref[...] / l_ref[...]
    a = o[0:tq] - lam * o[tq:2 * tq]
    a = _rms(a, subw_ref[...]) * (1.0 - lam_init)
    o_ref[0] = a.astype(BF16)


def _diff_attn(proj3, lam_rows, subw, lam_init, tq):
    b, s, _ = proj3.shape
    kern = functools.partial(_diff_attn_kernel, tq=tq, lam_init=lam_init)
    return pl.pallas_call(
        kern,
        grid=(b, DIFF_HEADS, s // tq),
        in_specs=[_resident((4, DIFF_HEAD_DIM)),
                  _resident((1, LANES)),
                  pl.BlockSpec((1, tq, LANES), lambda bi, h, qi: (bi, qi, COL_DQ // LANES + h)),
                  pl.BlockSpec((1, s, LANES), lambda bi, h, qi: (bi, 0, COL_DK // LANES + h)),
                  pl.BlockSpec((1, s, LANES), lambda bi, h, qi: (bi, 0, COL_DV // LANES + h))],
        out_specs=pl.BlockSpec((1, tq, LANES), lambda bi, h, qi: (bi, qi, h)),
        out_shape=jax.ShapeDtypeStruct((b, s, DIFF_WIDTH), BF16),
        scratch_shapes=[pltpu.VMEM((2 * tq, LANES), BF16),
                        pltpu.VMEM((2 * tq, 1), F32),
                        pltpu.VMEM((2 * tq, 1), F32),
                        pltpu.VMEM((2 * tq, LANES), F32)],
        compiler_params=_cparams(("parallel", "parallel", "parallel")),
        name="diff_attn",
    )(lam_rows, subw, proj3, proj3, proj3)


def _mla_attn_kernel(q_ref, k_ref, v_ref, o_ref, qh_ref, m_ref, l_ref, acc_ref, *, tq):
    qi = pl.program_id(2)
    outs = []
    for hh in range(2):
        sl = slice(hh * LANES, (hh + 1) * LANES)
        qh_ref[...] = q_ref[0, :, sl]
        _flash_causal(qh_ref,
                      lambda st, sl=sl: k_ref[0, pl.ds(st, tq), sl],
                      lambda st: v_ref[0, pl.ds(st, tq), :],
                      m_ref, l_ref, acc_ref, qi, tq)
        outs.append(acc_ref[...] / l_ref[...])
    lane = lax.broadcasted_iota(jnp.int32, outs[0].shape, 1)
    o_ref[0] = jnp.where(lane < MLA_V_DIM, outs[0], outs[1]).astype(BF16)


def _mla_attn(q3, k3, v3, tq):
    b, s, _ = q3.shape
    kern = functools.partial(_mla_attn_kernel, tq=tq)
    return pl.pallas_call(
        kern,
        grid=(b, MLA_HEADS // 2, s // tq),
        in_specs=[pl.BlockSpec((1, tq, 2 * LANES), lambda bi, h, qi: (bi, qi, h)),
                  pl.BlockSpec((1, s, 2 * LANES), lambda bi, h, qi: (bi, 0, h)),
                  pl.BlockSpec((1, s, LANES), lambda bi, h, qi: (bi, 0, h))],
        out_specs=pl.BlockSpec((1, tq, LANES), lambda bi, h, qi: (bi, qi, h)),
        out_shape=jax.ShapeDtypeStruct((b, s, MLA_V_WIDTH), BF16),
        scratch_shapes=[pltpu.VMEM((tq, LANES), BF16),
                        pltpu.VMEM((tq, 1), F32),
                        pltpu.VMEM((tq, 1), F32),
                        pltpu.VMEM((tq, LANES), F32)],
        compiler_params=_cparams(("parallel", "parallel", "parallel")),
        name="mla_attn",
    )(q3, k3, v3)


def _mem_kv_kernel(m_ref, g_ref, w_ref, o_ref):
    h = _rms(m_ref[...], g_ref[...]).astype(BF16)
    o_ref[...] = jnp.dot(h, w_ref[...], preferred_element_type=F32).astype(BF16)


def _mem_kv(mem2, gain, w, tm):
    t = mem2.shape[0]
    return pl.pallas_call(
        _mem_kv_kernel,
        grid=(t // tm,),
        in_specs=[pl.BlockSpec((tm, D_MODEL), lambda i: (i, 0)),
                  _resident((1, D_MODEL)),
                  _resident((D_MODEL, 2 * CROSS_WIDTH))],
        out_specs=pl.BlockSpec((tm, 2 * CROSS_WIDTH), lambda i: (i, 0)),
        out_shape=jax.ShapeDtypeStruct((t, 2 * CROSS_WIDTH), BF16),
        compiler_params=_cparams(("parallel",)),
        name="mem_kv",
    )(mem2, gain, w)


def _cross_attn_kernel(q_ref, k_ref, v_ref, o_ref):
    for hd in range(CROSS_HEADS):
        sl = slice(hd * LANES, (hd + 1) * LANES)
        s = lax.dot_general(q_ref[0, :, sl], k_ref[0, :, sl], (((1,), (1,)), ((), ())),
                            preferred_element_type=F32)
        p = jnp.exp(s - jnp.max(s, axis=1, keepdims=True))
        l = jnp.sum(p, axis=1, keepdims=True)
        o = jnp.dot(p.astype(BF16), v_ref[0, :, sl], preferred_element_type=F32)
        o_ref[0, :, sl] = (o / l).astype(BF16)


def _cross_attn(proj3, memkv3, tq):
    b, s, _ = proj3.shape
    m_len = memkv3.shape[1]
    return pl.pallas_call(
        _cross_attn_kernel,
        grid=(b, s // tq),
        in_specs=[pl.BlockSpec((1, tq, CROSS_WIDTH), lambda bi, qi: (bi, qi, COL_XQ // CROSS_WIDTH)),
                  pl.BlockSpec((1, m_len, CROSS_WIDTH), lambda bi, qi: (bi, 0, 0)),
                  pl.BlockSpec((1, m_len, CROSS_WIDTH), lambda bi, qi: (bi, 0, 1))],
        out_specs=pl.BlockSpec((1, tq, CROSS_WIDTH), lambda bi, qi: (bi, qi, 0)),
        out_shape=jax.ShapeDtypeStruct((b, s, CROSS_WIDTH), BF16),
        compiler_params=_cparams(("parallel", "parallel")),
        name="cross_attn",
    )(proj3, memkv3, memkv3)


def _merge_kernel(x_ref, a_ref, mo_ref, co_ref, g_ref, wa_ref, wb_ref, wc_ref, wo_ref, o_ref):
    ya = jnp.dot(a_ref[...], wa_ref[...], preferred_element_type=F32)
    m = g_ref[:, 0:D_MODEL].astype(F32) * ya
    yb = jnp.dot(mo_ref[...], wb_ref[...], preferred_element_type=F32)
    m = m + g_ref[:, D_MODEL:2 * D_MODEL].astype(F32) * yb
    yc = jnp.dot(co_ref[...], wc_ref[...], preferred_element_type=F32)
    m = m + g_ref[:, 2 * D_MODEL:3 * D_MODEL].astype(F32) * yc
    o_ref[...] = x_ref[...] + jnp.dot(m.astype(BF16), wo_ref[...], preferred_element_type=F32)


def _merge(x2, a2, mo2, co2, proj, wa, wb, wc, wo, tm):
    t = x2.shape[0]
    return pl.pallas_call(
        _merge_kernel,
        grid=(t // tm,),
        in_specs=[pl.BlockSpec((tm, D_MODEL), lambda i: (i, 0)),
                  pl.BlockSpec((tm, DIFF_WIDTH), lambda i: (i, 0)),
                  pl.BlockSpec((tm, MLA_V_WIDTH), lambda i: (i, 0)),
                  pl.BlockSpec((tm, CROSS_WIDTH), lambda i: (i, 0)),
                  pl.BlockSpec((tm, 3 * D_MODEL), lambda i: (i, 0)),
                  _resident((DIFF_WIDTH, D_MODEL)),
                  _resident((MLA_V_WIDTH, D_MODEL)),
                  _resident((CROSS_WIDTH, D_MODEL)),
                  _resident((D_MODEL, D_MODEL))],
        out_specs=pl.BlockSpec((tm, D_MODEL), lambda i: (i, 0)),
        out_shape=jax.ShapeDtypeStruct((t, D_MODEL), F32),
        compiler_params=_cparams(("parallel",)),
        name="merge_out",
    )(x2, a2, mo2, co2, proj, wa, wb, wc, wo)


def _ffn_kernel(x_ref, g_ref, wgu_ref, wd_ref, fg_ref, o_ref, h_ref, act_ref, *, final_norm):
    x = x_ref[...]
    h_ref[...] = _rms(x, g_ref[...]).astype(BF16)
    hc = FFN_HIDDEN // 11
    for c0 in range(0, FFN_HIDDEN, hc):
        gate = jnp.dot(h_ref[...], wgu_ref[:, c0:c0 + hc], preferred_element_type=F32)
        up = jnp.dot(h_ref[...], wgu_ref[:, FFN_HIDDEN + c0:FFN_HIDDEN + c0 + hc],
                     preferred_element_type=F32)
        act_ref[:, c0:c0 + hc] = (jax.nn.silu(gate) * up).astype(BF16)
    y = x + jnp.dot(act_ref[...], wd_ref[...], preferred_element_type=F32)
    if final_norm:
        y = _rms(y, fg_ref[...])
    o_ref[...] = y


def _ffn(x2, gain, wgu, wd, fgain, final_norm, tm):
    t = x2.shape[0]
    kern = functools.partial(_ffn_kernel, final_norm=final_norm)
    return pl.pallas_call(
        kern,
        grid=(t // tm,),
        in_specs=[pl.BlockSpec((tm, D_MODEL), lambda i: (i, 0)),
                  _resident((1, D_MODEL)),
                  _resident((D_MODEL, 2 * FFN_HIDDEN)),
                  _resident((FFN_HIDDEN, D_MODEL)),
                  _resident((1, D_MODEL))],
        out_specs=pl.BlockSpec((tm, D_MODEL), lambda i: (i, 0)),
        out_shape=jax.ShapeDtypeStruct((t, D_MODEL), F32),
        scratch_shapes=[pltpu.VMEM((tm, D_MODEL), BF16),
                        pltpu.VMEM((tm, FFN_HIDDEN), BF16)],
        compiler_params=_cparams(("parallel",)),
        name="ffn",
    )(x2, gain, wgu, wd, fgain)


def _pad_w_in(w):
    dq, dk, dv, cq, ckv, kr, xq, gates = jnp.split(
        w, (1024, 2048, 3072, 3584, 3840, 3872, 4384), axis=-1)
    z = lambda n: jnp.zeros(w.shape[:-1] + (n,), w.dtype)
    kr_group = jnp.concatenate([z(MLA_NOPE_DIM), kr, z(LANES - MLA_QK_DIM)], axis=-1)
    return jnp.concatenate([gates, dq, dk, dv, cq, ckv, kr_group, z(LANES), xq], axis=-1).astype(BF16)


def _pad_heads(w, n_heads, width):
    k = w.shape[-2]
    wh = w.reshape(w.shape[:-1] + (n_heads, width))
    wh = jnp.pad(wh, [(0, 0)] * (wh.ndim - 1) + [(0, LANES - width)])
    return wh.reshape(w.shape[:-2] + (k, n_heads * LANES)).astype(BF16)


def _lane_pattern(rot_dim, offset, period):
    half = rot_dim // 2
    inv_freq = ROPE_THETA ** (-jnp.arange(half, dtype=F32) * 2.0 / rot_dim)
    d = jnp.arange(LANES) % period - offset
    first = (d >= 0) & (d < half)
    second = (d >= half) & (d < rot_dim)
    freq = jnp.where(first | second, inv_freq[jnp.clip(d, 0, rot_dim - 1) % half], 0.0)
    sign_a = jnp.where(first, -1.0, 0.0)
    sign_b = jnp.where(second, 1.0, 0.0)
    row = lambda v: v.astype(F32).reshape(1, LANES)
    return row(freq), row(sign_a), row(sign_b)


def kernel(x, mem, positions, attn_norm_w, w_in, diff_lambda, diff_subln_w, w_o_diff,
           mla_q_norm_w, mla_w_uq, mla_kv_norm_w, mla_w_ukv, w_o_mla,
           mem_norm_w, w_mem_kv, w_o_cross, w_out, ffn_norm_w, w_gate_up, w_down,
           final_norm_w):
    b, s, _ = x.shape
    m_len = mem.shape[1]
    depth = w_in.shape[0]
    t = b * s
    tm = min(512, s)
    tq_diff = min(256, s)
    tq_mla = min(512, s)

    w_in_p = _pad_w_in(w_in)
    w_uq_p = _pad_heads(mla_w_uq, MLA_HEADS, MLA_QK_DIM)
    ukv = mla_w_ukv.reshape(depth, MLA_KV_LORA, MLA_HEADS, MLA_NOPE_DIM + MLA_V_DIM)
    w_uk_p = _pad_heads(ukv[..., :MLA_NOPE_DIM].reshape(depth, MLA_KV_LORA, -1), MLA_HEADS, MLA_NOPE_DIM)
    w_uv = ukv[..., MLA_NOPE_DIM:].reshape(depth, MLA_KV_LORA, MLA_V_WIDTH).astype(BF16)
    bf = lambda w: w.astype(BF16)
    w_o_diff_b, w_o_mla_b, w_o_cross_b, w_out_b = bf(w_o_diff), bf(w_o_mla), bf(w_o_cross), bf(w_out)
    w_mem_kv_b, w_gate_up_b, w_down_b = bf(w_mem_kv), bf(w_gate_up), bf(w_down)

    posf = positions.reshape(t, 1).astype(F32)
    tabs8 = _rope_tables(posf, *_lane_pattern(DIFF_ROT_DIM, 0, DIFF_HEAD_DIM), tm)
    tabs16 = _rope_tables(posf, *_lane_pattern(MLA_ROPE_DIM, MLA_NOPE_DIM, LANES), tm)

    x2 = x.reshape(t, D_MODEL)
    mem2 = mem.reshape(b * m_len, D_MODEL)
    row = lambda v: v.reshape(1, -1)
    for i in range(depth):
        lam_init = 0.8 - 0.6 * math.exp(-0.3 * i)
        proj = _inproj(x2, row(attn_norm_w[i]), w_in_p[i], tabs8, tabs16,
                       row(mla_q_norm_w[i]), row(mla_kv_norm_w[i]), tm)
        proj3 = proj.reshape(b, s, IN_COLS_PAD)
        a = _diff_attn(proj3, diff_lambda[i], row(diff_subln_w[i]), lam_init, tq_diff)
        qm, km, vm = _mla_up(proj, w_uq_p[i], w_uk_p[i], w_uv[i], tabs16, tm)
        mo = _mla_attn(qm.reshape(b, s, -1), km.reshape(b, s, -1), vm.reshape(b, s, -1), tq_mla)
        memkv = _mem_kv(mem2, row(mem_norm_w[i]), w_mem_kv_b[i], min(512, b * m_len))
        co = _cross_attn(proj3, memkv.reshape(b, m_len, 2 * CROSS_WIDTH), tm)
        x2 = _merge(x2, a.reshape(t, -1), mo.reshape(t, -1), co.reshape(t, -1), proj,
                    w_o_diff_b[i], w_o_mla_b[i], w_o_cross_b[i], w_out_b[i], tm)
        x2 = _ffn(x2, row(ffn_norm_w[i]), w_gate_up_b[i], w_down_b[i], row(final_norm_w),
                  i == depth - 1, tm)
    return x2.reshape(b, s, D_MODEL)
```

```python
import functools
import math

import jax
import jax.numpy as jnp
from jax import lax
from jax.experimental import pallas as pl
from jax.experimental.pallas import tpu as pltpu

F32 = jnp.float32
BF16 = jnp.bfloat16

D_MODEL = 1024
EPS = 1e-6
ROPE_THETA = 500000.0
LOG2E = math.log2(math.e)

DIFF_HEADS = 8
DIFF_HEAD_DIM = 64
DIFF_ROT_DIM = DIFF_HEAD_DIM // 4
DIFF_WIDTH = 1024

MLA_HEADS = 8
MLA_Q_LORA = 512
MLA_KV_LORA = 256
MLA_NOPE_DIM = 64
MLA_ROPE_DIM = 32
MLA_V_DIM = 64
MLA_QK_DIM = MLA_NOPE_DIM + MLA_ROPE_DIM
MLA_V_WIDTH = MLA_HEADS * MLA_V_DIM

CROSS_HEADS = 4
CROSS_HEAD_DIM = 128
CROSS_WIDTH = CROSS_HEADS * CROSS_HEAD_DIM

FFN_HIDDEN = 2816

LANES = 128
CHUNK = 512

COL_GATES = 0
COL_DQ = 3072
COL_DK = 4096
COL_DV = 5120
COL_CQ = 6144
COL_CKV = 6656
COL_XQ = 7168
IN_COLS_PAD = 7680

VMEM_LIMIT = 56 * 1024 * 1024


def _cparams(sem):
    return pltpu.CompilerParams(dimension_semantics=sem, vmem_limit_bytes=VMEM_LIMIT)


def _resident(shape):
    nd = len(shape)
    return pl.BlockSpec(shape, lambda *_: (0,) * nd, pipeline_mode=pl.Buffered(1))


def _rms(x, w):
    y = x * lax.rsqrt(jnp.mean(x * x, axis=-1, keepdims=True) + EPS)
    return y * w


def _rope_table_kernel(pos_ref, freq_ref, sa_ref, sb_ref, c_ref, a_ref, b_ref):
    ang = pos_ref[...] * freq_ref[...]
    c = jnp.cos(ang)
    s = jnp.sin(ang)
    c_ref[...] = c
    a_ref[...] = s * sa_ref[...]
    b_ref[...] = s * sb_ref[...]


def _rope_tables(posf, freq, sign_a, sign_b, tm):
    t = posf.shape[0]
    tok = pl.BlockSpec((tm, LANES), lambda i: (i, 0))
    row = pl.BlockSpec((1, LANES), lambda i: (0, 0))
    out = jax.ShapeDtypeStruct((t, LANES), F32)
    return pl.pallas_call(
        _rope_table_kernel,
        grid=(t // tm,),
        in_specs=[pl.BlockSpec((tm, 1), lambda i: (i, 0)), row, row, row],
        out_specs=[tok, tok, tok],
        out_shape=[out, out, out],
        compiler_params=_cparams(("parallel",)),
        name="rope_tables",
    )(posf, freq, sign_a, sign_b)


def _rope128(x, c, a, b, half):
    return x * c + pltpu.roll(x, LANES - half, 1) * a + pltpu.roll(x, half, 1) * b


def _inproj_kernel(x_ref, g_ref, w_ref, c8_ref, a8_ref, b8_ref, c16_ref, a16_ref, b16_ref,
                   qn_ref, kvn_ref, o_ref, h_ref):
    h_ref[...] = _rms(x_ref[...], g_ref[...]).astype(BF16)
    for c0 in range(0, IN_COLS_PAD, CHUNK):
        acc = jnp.dot(h_ref[...], w_ref[:, c0:c0 + CHUNK], preferred_element_type=F32)
        if c0 < COL_DQ:
            o_ref[:, c0:c0 + CHUNK] = jax.nn.sigmoid(acc).astype(BF16)
        elif c0 < COL_DV:
            scale = DIFF_HEAD_DIM ** -0.5 * LOG2E if c0 < COL_DK else 1.0
            for s0 in range(0, CHUNK, LANES):
                r = _rope128(acc[:, s0:s0 + LANES], c8_ref[...], a8_ref[...], b8_ref[...],
                             DIFF_ROT_DIM // 2)
                o_ref[:, c0 + s0:c0 + s0 + LANES] = (r * scale).astype(BF16)
        elif c0 == COL_CQ:
            o_ref[:, c0:c0 + CHUNK] = _rms(acc, qn_ref[...]).astype(BF16)
        elif c0 == COL_CKV:
            o_ref[:, c0:c0 + MLA_KV_LORA] = _rms(acc[:, :MLA_KV_LORA], kvn_ref[...]).astype(BF16)
            kr = _rope128(acc[:, MLA_KV_LORA:MLA_KV_LORA + LANES], c16_ref[...], a16_ref[...],
                          b16_ref[...], MLA_ROPE_DIM // 2)
            o_ref[:, c0 + MLA_KV_LORA:c0 + MLA_KV_LORA + LANES] = kr.astype(BF16)
            o_ref[:, c0 + MLA_KV_LORA + LANES:c0 + CHUNK] = acc[:, MLA_KV_LORA + LANES:].astype(BF16)
        elif c0 == COL_XQ:
            o_ref[:, c0:c0 + CHUNK] = (acc * (CROSS_HEAD_DIM ** -0.5 * LOG2E)).astype(BF16)
        else:
            o_ref[:, c0:c0 + CHUNK] = acc.astype(BF16)


def _inproj(x2, gain, w, tabs8, tabs16, qn, kvn, tm):
    t = x2.shape[0]
    tok = pl.BlockSpec((tm, LANES), lambda i: (i, 0))
    return pl.pallas_call(
        _inproj_kernel,
        grid=(t // tm,),
        in_specs=[pl.BlockSpec((tm, D_MODEL), lambda i: (i, 0)),
                  _resident((1, D_MODEL)),
                  _resident((D_MODEL, IN_COLS_PAD)),
                  tok, tok, tok, tok, tok, tok,
                  _resident((1, MLA_Q_LORA)),
                  _resident((1, MLA_KV_LORA))],
        out_specs=pl.BlockSpec((tm, IN_COLS_PAD), lambda i: (i, 0)),
        out_shape=jax.ShapeDtypeStruct((t, IN_COLS_PAD), BF16),
        scratch_shapes=[pltpu.VMEM((tm, D_MODEL), BF16)],
        compiler_params=_cparams(("parallel",)),
        name="inproj",
    )(x2, gain, w, *tabs8, *tabs16, qn, kvn)


def _mla_up_kernel(cq_ref, ckv_ref, wq_ref, wk_ref, wv_ref, c_ref, a_ref, b_ref,
                   q_ref, k_ref, v_ref):
    scale = MLA_QK_DIM ** -0.5 * LOG2E
    q = jnp.dot(cq_ref[...], wq_ref[...], preferred_element_type=F32)
    lat = ckv_ref[:, :MLA_KV_LORA]
    kr = ckv_ref[:, MLA_KV_LORA:MLA_KV_LORA + LANES]
    k = jnp.dot(lat, wk_ref[...], preferred_element_type=F32)
    for hd in range(MLA_HEADS):
        sl = slice(hd * LANES, (hd + 1) * LANES)
        qh = _rope128(q[:, sl], c_ref[...], a_ref[...], b_ref[...], MLA_ROPE_DIM // 2)
        q_ref[:, sl] = (qh * scale).astype(BF16)
        k_ref[:, sl] = (k[:, sl] + kr.astype(F32)).astype(BF16)
    v_ref[...] = jnp.dot(lat, wv_ref[...], preferred_element_type=F32).astype(BF16)


def _mla_up(proj, wq, wk, wv, tabs16, tm):
    t = proj.shape[0]
    tok = pl.BlockSpec((tm, LANES), lambda i: (i, 0))
    wide = MLA_HEADS * LANES
    return pl.pallas_call(
        _mla_up_kernel,
        grid=(t // tm,),
        in_specs=[pl.BlockSpec((tm, CHUNK), lambda i: (i, COL_CQ // CHUNK)),
                  pl.BlockSpec((tm, CHUNK), lambda i: (i, COL_CKV // CHUNK)),
                  _resident((MLA_Q_LORA, wide)),
                  _resident((MLA_KV_LORA, wide)),
                  _resident((MLA_KV_LORA, MLA_V_WIDTH)),
                  tok, tok, tok],
        out_specs=[pl.BlockSpec((tm, wide), lambda i: (i, 0)),
                   pl.BlockSpec((tm, wide), lambda i: (i, 0)),
                   pl.BlockSpec((tm, MLA_V_WIDTH), lambda i: (i, 0))],
        out_shape=[jax.ShapeDtypeStruct((t, wide), BF16),
                   jax.ShapeDtypeStruct((t, wide), BF16),
                   jax.ShapeDtypeStruct((t, MLA_V_WIDTH), BF16)],
        compiler_params=_cparams(("parallel",)),
        name="mla_up",
    )(proj, proj, wq, wk, wv, *tabs16)


def _causal_rows(q, k_ref, v_ref, kcols, qi, tq):
    rows = q.shape[0]
    n0 = qi * tq
    nt = (((1,), (1,)), ((), ()))
    s_d = lax.dot_general(q, k_ref[0, n0:n0 + tq, kcols], nt, preferred_element_type=F32)
    r = lax.broadcasted_iota(jnp.int32, (rows, tq), 0) & (tq - 1)
    c = lax.broadcasted_iota(jnp.int32, (rows, tq), 1)
    s_d = jnp.where(r >= c, s_d, -1e30)
    m = jnp.max(s_d, axis=1, keepdims=True)
    if qi > 0:
        s_o = lax.dot_general(q, k_ref[0, 0:n0, kcols], nt, preferred_element_type=F32)
        m = jnp.maximum(m, jnp.max(s_o, axis=1, keepdims=True))
    p_d = jnp.exp2(s_d - m)
    l = jnp.sum(p_d, axis=1, keepdims=True)
    o = jnp.dot(p_d.astype(BF16), v_ref[0, n0:n0 + tq, :], preferred_element_type=F32)
    if qi > 0:
        p_o = jnp.exp2(s_o - m)
        l = l + jnp.sum(p_o, axis=1, keepdims=True)
        o = o + jnp.dot(p_o.astype(BF16), v_ref[0, 0:n0, :], preferred_element_type=F32)
    return o / l


def _diff_attn_kernel(lam_ref, subw_ref, q_ref, k_ref, v_ref, o_ref, *, tq, lam_init):
    lv = lam_ref[...]
    lam = (jnp.exp(jnp.sum(lv[0:1] * lv[1:2], axis=1, keepdims=True))
           - jnp.exp(jnp.sum(lv[2:3] * lv[3:4], axis=1, keepdims=True)) + lam_init)
    lane = lax.broadcasted_iota(jnp.int32, (tq, LANES), 1)
    for qi in range(q_ref.shape[1] // tq):
        q = q_ref[0, qi * tq:(qi + 1) * tq, :]
        zero = jnp.zeros_like(q)
        q2 = jnp.concatenate([jnp.where(lane < DIFF_HEAD_DIM, q, zero),
                              jnp.where(lane >= DIFF_HEAD_DIM, q, zero)], axis=0)
        o = _causal_rows(q2, k_ref, v_ref, slice(None), qi, tq)
        a = o[0:tq] - lam * o[tq:2 * tq]
        a = _rms(a, subw_ref[...]) * (1.0 - lam_init)
        o_ref[0, qi * tq:(qi + 1) * tq, :] = a.astype(BF16)


def _diff_attn(proj3, lam_rows, subw, lam_init, tq):
    b, s, _ = proj3.shape
    kern = functools.partial(_diff_attn_kernel, tq=tq, lam_init=lam_init)
    seq = lambda col: pl.BlockSpec((1, s, LANES), lambda bi, h: (bi, 0, col // LANES + h))
    return pl.pallas_call(
        kern,
        grid=(b, DIFF_HEADS),
        in_specs=[_resident((4, DIFF_HEAD_DIM)), _resident((1, LANES)),
                  seq(COL_DQ), seq(COL_DK), seq(COL_DV)],
        out_specs=seq(0),
        out_shape=jax.ShapeDtypeStruct((b, s, DIFF_WIDTH), BF16),
        compiler_params=_cparams(("parallel", "parallel")),
        name="diff_attn",
    )(lam_rows, subw, proj3, proj3, proj3)


def _mla_attn_kernel(q_ref, k_ref, v_ref, o_ref, *, tq):
    lane = lax.broadcasted_iota(jnp.int32, (tq, LANES), 1)
    for qi in range(q_ref.shape[1] // tq):
        outs = []
        for hh in range(2):
            sl = slice(hh * LANES, (hh + 1) * LANES)
            outs.append(_causal_rows(q_ref[0, qi * tq:(qi + 1) * tq, sl], k_ref, v_ref, sl, qi, tq))
        o_ref[0, qi * tq:(qi + 1) * tq, :] = jnp.where(lane < MLA_V_DIM, outs[0], outs[1]).astype(BF16)


def _mla_attn(q3, k3, v3, tq):
    b, s, _ = q3.shape
    kern = functools.partial(_mla_attn_kernel, tq=tq)
    return pl.pallas_call(
        kern,
        grid=(b, MLA_HEADS // 2),
        in_specs=[pl.BlockSpec((1, s, 2 * LANES), lambda bi, h: (bi, 0, h)),
                  pl.BlockSpec((1, s, 2 * LANES), lambda bi, h: (bi, 0, h)),
                  pl.BlockSpec((1, s, LANES), lambda bi, h: (bi, 0, h))],
        out_specs=pl.BlockSpec((1, s, LANES), lambda bi, h: (bi, 0, h)),
        out_shape=jax.ShapeDtypeStruct((b, s, MLA_V_WIDTH), BF16),
        compiler_params=_cparams(("parallel", "parallel")),
        name="mla_attn",
    )(q3, k3, v3)


def _mem_kv_kernel(m_ref, g_ref, w_ref, o_ref):
    h = _rms(m_ref[...], g_ref[...]).astype(BF16)
    o_ref[...] = jnp.dot(h, w_ref[...], preferred_element_type=F32).astype(BF16)


def _mem_kv(mem2, gain, w, tm):
    t = mem2.shape[0]
    return pl.pallas_call(
        _mem_kv_kernel,
        grid=(t // tm,),
        in_specs=[pl.BlockSpec((tm, D_MODEL), lambda i: (i, 0)),
                  _resident((1, D_MODEL)),
                  _resident((D_MODEL, 2 * CROSS_WIDTH))],
        out_specs=pl.BlockSpec((tm, 2 * CROSS_WIDTH), lambda i: (i, 0)),
        out_shape=jax.ShapeDtypeStruct((t, 2 * CROSS_WIDTH), BF16),
        compiler_params=_cparams(("parallel",)),
        name="mem_kv",
    )(mem2, gain, w)


def _cross_attn_kernel(q_ref, k_ref, v_ref, o_ref):
    for hd in range(CROSS_HEADS):
        sl = slice(hd * LANES, (hd + 1) * LANES)
        s = lax.dot_general(q_ref[0, :, sl], k_ref[0, :, sl], (((1,), (1,)), ((), ())),
                            preferred_element_type=F32)
        p = jnp.exp2(s - jnp.max(s, axis=1, keepdims=True))
        l = jnp.sum(p, axis=1, keepdims=True)
        o = jnp.dot(p.astype(BF16), v_ref[0, :, sl], preferred_element_type=F32)
        o_ref[0, :, sl] = (o / l).astype(BF16)


def _cross_attn(proj3, memkv3, tq):
    b, s, _ = proj3.shape
    m_len = memkv3.shape[1]
    return pl.pallas_call(
        _cross_attn_kernel,
        grid=(b, s // tq),
        in_specs=[pl.BlockSpec((1, tq, CROSS_WIDTH), lambda bi, qi: (bi, qi, COL_XQ // CROSS_WIDTH)),
                  pl.BlockSpec((1, m_len, CROSS_WIDTH), lambda bi, qi: (bi, 0, 0)),
                  pl.BlockSpec((1, m_len, CROSS_WIDTH), lambda bi, qi: (bi, 0, 1))],
        out_specs=pl.BlockSpec((1, tq, CROSS_WIDTH), lambda bi, qi: (bi, qi, 0)),
        out_shape=jax.ShapeDtypeStruct((b, s, CROSS_WIDTH), BF16),
        compiler_params=_cparams(("parallel", "parallel")),
        name="cross_attn",
    )(proj3, memkv3, memkv3)


def _merge_kernel(x_ref, a_ref, mo_ref, co_ref, g_ref, wa_ref, wb_ref, wc_ref, wo_ref, o_ref):
    ya = jnp.dot(a_ref[...], wa_ref[...], preferred_element_type=F32)
    m = g_ref[:, 0:D_MODEL].astype(F32) * ya
    yb = jnp.dot(mo_ref[...], wb_ref[...], preferred_element_type=F32)
    m = m + g_ref[:, D_MODEL:2 * D_MODEL].astype(F32) * yb
    yc = jnp.dot(co_ref[...], wc_ref[...], preferred_element_type=F32)
    m = m + g_ref[:, 2 * D_MODEL:3 * D_MODEL].astype(F32) * yc
    o_ref[...] = x_ref[...] + jnp.dot(m.astype(BF16), wo_ref[...], preferred_element_type=F32)


def _merge(x2, a2, mo2, co2, proj, wa, wb, wc, wo, tm):
    t = x2.shape[0]
    return pl.pallas_call(
        _merge_kernel,
        grid=(t // tm,),
        in_specs=[pl.BlockSpec((tm, D_MODEL), lambda i: (i, 0)),
                  pl.BlockSpec((tm, DIFF_WIDTH), lambda i: (i, 0)),
                  pl.BlockSpec((tm, MLA_V_WIDTH), lambda i: (i, 0)),
                  pl.BlockSpec((tm, CROSS_WIDTH), lambda i: (i, 0)),
                  pl.BlockSpec((tm, 3 * D_MODEL), lambda i: (i, 0)),
                  _resident((DIFF_WIDTH, D_MODEL)),
                  _resident((MLA_V_WIDTH, D_MODEL)),
                  _resident((CROSS_WIDTH, D_MODEL)),
                  _resident((D_MODEL, D_MODEL))],
        out_specs=pl.BlockSpec((tm, D_MODEL), lambda i: (i, 0)),
        out_shape=jax.ShapeDtypeStruct((t, D_MODEL), F32),
        compiler_params=_cparams(("parallel",)),
        name="merge_out",
    )(x2, a2, mo2, co2, proj, wa, wb, wc, wo)


def _ffn_kernel(x_ref, g_ref, wgu_ref, wd_ref, fg_ref, o_ref, h_ref, act_ref, *, final_norm):
    x = x_ref[...]
    h_ref[...] = _rms(x, g_ref[...]).astype(BF16)
    hc = FFN_HIDDEN // 11
    for c0 in range(0, FFN_HIDDEN, hc):
        gate = jnp.dot(h_ref[...], wgu_ref[:, c0:c0 + hc], preferred_element_type=F32)
        up = jnp.dot(h_ref[...], wgu_ref[:, FFN_HIDDEN + c0:FFN_HIDDEN + c0 + hc],
                     preferred_element_type=F32)
        act_ref[:, c0:c0 + hc] = (jax.nn.silu(gate) * up).astype(BF16)
    y = x + jnp.dot(act_ref[...], wd_ref[...], preferred_element_type=F32)
    if final_norm:
        y = _rms(y, fg_ref[...])
    o_ref[...] = y


def _ffn(x2, gain, wgu, wd, fgain, final_norm, tm):
    t = x2.shape[0]
    kern = functools.partial(_ffn_kernel, final_norm=final_norm)
    return pl.pallas_call(
        kern,
        grid=(t // tm,),
        in_specs=[pl.BlockSpec((tm, D_MODEL), lambda i: (i, 0)),
                  _resident((1, D_MODEL)),
                  _resident((D_MODEL, 2 * FFN_HIDDEN)),
                  _resident((FFN_HIDDEN, D_MODEL)),
                  _resident((1, D_MODEL))],
        out_specs=pl.BlockSpec((tm, D_MODEL), lambda i: (i, 0)),
        out_shape=jax.ShapeDtypeStruct((t, D_MODEL), F32),
        scratch_shapes=[pltpu.VMEM((tm, D_MODEL), BF16),
                        pltpu.VMEM((tm, FFN_HIDDEN), BF16)],
        compiler_params=_cparams(("parallel",)),
        name="ffn",
    )(x2, gain, wgu, wd, fgain)


def _pad_w_in(w):
    dq, dk, dv, cq, ckv, kr, xq, gates = jnp.split(
        w, (1024, 2048, 3072, 3584, 3840, 3872, 4384), axis=-1)
    z = lambda n: jnp.zeros(w.shape[:-1] + (n,), w.dtype)
    kr_group = jnp.concatenate([z(MLA_NOPE_DIM), kr, z(LANES - MLA_QK_DIM)], axis=-1)
    return jnp.concatenate([gates, dq, dk, dv, cq, ckv, kr_group, z(LANES), xq], axis=-1).astype(BF16)


def _pad_heads(w, n_heads, width):
    k = w.shape[-2]
    wh = w.reshape(w.shape[:-1] + (n_heads, width))
    wh = jnp.pad(wh, [(0, 0)] * (wh.ndim - 1) + [(0, LANES - width)])
    return wh.reshape(w.shape[:-2] + (k, n_heads * LANES)).astype(BF16)


def _lane_pattern(rot_dim, offset, period):
    half = rot_dim // 2
    inv_freq = ROPE_THETA ** (-jnp.arange(half, dtype=F32) * 2.0 / rot_dim)
    d = jnp.arange(LANES) % period - offset
    first = (d >= 0) & (d < half)
    second = (d >= half) & (d < rot_dim)
    freq = jnp.where(first | second, inv_freq[jnp.clip(d, 0, rot_dim - 1) % half], 0.0)
    sign_a = jnp.where(first, -1.0, 0.0)
    sign_b = jnp.where(second, 1.0, 0.0)
    row = lambda v: v.astype(F32).reshape(1, LANES)
    return row(freq), row(sign_a), row(sign_b)


def kernel(x, mem, positions, attn_norm_w, w_in, diff_lambda, diff_subln_w, w_o_diff,
           mla_q_norm_w, mla_w_uq, mla_kv_norm_w, mla_w_ukv, w_o_mla,
           mem_norm_w, w_mem_kv, w_o_cross, w_out, ffn_norm_w, w_gate_up, w_down,
           final_norm_w):
    b, s, _ = x.shape
    m_len = mem.shape[1]
    depth = w_in.shape[0]
    t = b * s
    tm = min(512, s)
    tq_diff = min(256, s)
    tq_mla = min(512, s)

    w_in_p = _pad_w_in(w_in)
    w_uq_p = _pad_heads(mla_w_uq, MLA_HEADS, MLA_QK_DIM)
    ukv = mla_w_ukv.reshape(depth, MLA_KV_LORA, MLA_HEADS, MLA_NOPE_DIM + MLA_V_DIM)
    w_uk_p = _pad_heads(ukv[..., :MLA_NOPE_DIM].reshape(depth, MLA_KV_LORA, -1), MLA_HEADS, MLA_NOPE_DIM)
    w_uv = ukv[..., MLA_NOPE_DIM:].reshape(depth, MLA_KV_LORA, MLA_V_WIDTH).astype(BF16)
    bf = lambda w: w.astype(BF16)
    w_o_diff_b, w_o_mla_b, w_o_cross_b, w_out_b = bf(w_o_diff), bf(w_o_mla), bf(w_o_cross), bf(w_out)
    w_mem_kv_b, w_gate_up_b, w_down_b = bf(w_mem_kv), bf(w_gate_up), bf(w_down)

    posf = positions.reshape(t, 1).astype(F32)
    tabs8 = _rope_tables(posf, *_lane_pattern(DIFF_ROT_DIM, 0, DIFF_HEAD_DIM), tm)
    tabs16 = _rope_tables(posf, *_lane_pattern(MLA_ROPE_DIM, MLA_NOPE_DIM, LANES), tm)

    x2 = x.reshape(t, D_MODEL)
    mem2 = mem.reshape(b * m_len, D_MODEL)
    row = lambda v: v.reshape(1, -1)
    for i in range(depth):
        lam_init = 0.8 - 0.6 * math.exp(-0.3 * i)
        proj = _inproj(x2, row(attn_norm_w[i]), w_in_p[i], tabs8, tabs16,
                       row(mla_q_norm_w[i]), row(mla_kv_norm_w[i]), tm)
        proj3 = proj.reshape(b, s, IN_COLS_PAD)
        a = _diff_attn(proj3, diff_lambda[i], row(diff_subln_w[i]), lam_init, tq_diff)
        qm, km, vm = _mla_up(proj, w_uq_p[i], w_uk_p[i], w_uv[i], tabs16, tm)
        mo = _mla_attn(qm.reshape(b, s, -1), km.reshape(b, s, -1), vm.reshape(b, s, -1), tq_mla)
        memkv = _mem_kv(mem2, row(mem_norm_w[i]), w_mem_kv_b[i], min(512, b * m_len))
        co = _cross_attn(proj3, memkv.reshape(b, m_len, 2 * CROSS_WIDTH), tm)
        x2 = _merge(x2, a.reshape(t, -1), mo.reshape(t, -1), co.reshape(t, -1), proj,
                    w_o_diff_b[i], w_o_mla_b[i], w_o_cross_b[i], w_out_b[i], tm)
        x2 = _ffn(x2, row(ffn_norm_w[i]), w_gate_up_b[i], w_down_b[i], row(final_norm_w),
                  i == depth - 1, tm)
    return x2.reshape(b, s, D_MODEL)
```
